```python
import math
import jax, jax.numpy as jnp
from jax import lax
import numpy as np

D_MODEL = 1024
BATCH = 8
SEQ = 4096
DEPTH = 2

PLE_DIM = 256
NORM_EPS = 1e-6
D_MIX = D_MODEL
GM_HEADS = 4
GM_HEAD_DIM = 64
GM_WIDTH = GM_HEADS * GM_HEAD_DIM
GM_CHUNK = 128
LRU_HEADS = 4
LRU_HEAD_DIM = 64
LRU_WIDTH = LRU_HEADS * LRU_HEAD_DIM
LRU_CONV = 4
LRU_C = 8.0
DN_HEADS = 4
DN_HEAD_DIM = 128
DN_WIDTH = DN_HEADS * DN_HEAD_DIM
DN_CONV = 4
DN_CHUNK = 64
IN_SIZES = (GM_WIDTH, GM_WIDTH, LRU_WIDTH, LRU_WIDTH, DN_WIDTH, DN_WIDTH, DN_WIDTH, DN_WIDTH, DN_HEADS, DN_HEADS)
IN_WIDTH = 2 * GM_WIDTH + 2 * LRU_WIDTH + 4 * DN_WIDTH + 2 * DN_HEADS
PEER_HEADS = 8
PEER_NKEYS = 128
PEER_N = PEER_NKEYS * PEER_NKEYS
PEER_QDIM = 256
PEER_TOPK = 16
PEER_BLOCK = 128

kernel_name = 'hybrid_gmlp_rglru_gdn_peer'


def _rms_norm(x, g):
    xf = x.astype(jnp.float32)
    y = xf * lax.rsqrt(jnp.mean(xf * xf, axis=-1, keepdims=True) + NORM_EPS)
    return (y * g.astype(jnp.float32)).astype(x.dtype)


def _causal_dwconv(x, w):
    c = x.shape[-1]
    return lax.conv_general_dilated(x, w.astype(x.dtype)[:, None, :], window_strides=(1,),
                                    padding=[(w.shape[0] - 1, 0)],
                                    dimension_numbers=('NWC', 'WIO', 'NWC'),
                                    feature_group_count=c)


def _gmlp_mixer(u, v, ln_g, ln_b, w_s, b_s):
    bsz, s, _ = v.shape
    u = jax.nn.gelu(u)
    vf = jax.nn.gelu(v).astype(jnp.float32)
    mu = jnp.mean(vf, axis=-1, keepdims=True)
    var = jnp.mean(jnp.square(vf - mu), axis=-1, keepdims=True)
    vf = (vf - mu) * lax.rsqrt(var + NORM_EPS) * ln_g + ln_b
    vh = vf.reshape(bsz, s // GM_CHUNK, GM_CHUNK, GM_HEADS, GM_HEAD_DIM)
    causal = jnp.tril(jnp.ones((GM_CHUNK, GM_CHUNK), jnp.float32))
    ws = w_s.astype(jnp.float32) * causal[None]
    sv = jnp.einsum('hts,bcshd->bcthd', ws, vh) + b_s.astype(jnp.float32).T[:, :, None]
    return (u.astype(jnp.float32) * sv.reshape(bsz, s, GM_WIDTH)).astype(u.dtype)


def _lru_combine(left, right):
    a_l, b_l = left
    a_r, b_r = right
    return a_l * a_r, a_r * b_l + b_r


def _rglru_mixer(gate_in, x_in, conv_w, conv_b, wa, ba, wx, bx, lam):
    bsz, s, _ = x_in.shape
    gate = jax.nn.gelu(gate_in)
    xc = (_causal_dwconv(x_in, conv_w) + conv_b.astype(x_in.dtype)).astype(jnp.float32)
    xh = xc.reshape(bsz, s, LRU_HEADS, LRU_HEAD_DIM)
    r = jax.nn.sigmoid(jnp.einsum('bshi,hij->bshj', xh, wa.astype(jnp.float32)).reshape(bsz, s, LRU_WIDTH) + ba)
    i = jax.nn.sigmoid(jnp.einsum('bshi,hij->bshj', xh, wx.astype(jnp.float32)).reshape(bsz, s, LRU_WIDTH) + bx)
    log_a = -LRU_C * r * jax.nn.softplus(-lam.astype(jnp.float32))
    a = jnp.exp(log_a)
    mult = jnp.sqrt(-jnp.expm1(2.0 * log_a))
    mult = mult.at[:, 0].set(1.0)
    _, h = lax.associative_scan(_lru_combine, (a, mult * i * xc), axis=1)
    return h.astype(gate.dtype) * gate


def _to_chunks(t, nc):
    bsz, s, hh, d = t.shape
    return t.reshape(bsz, nc, DN_CHUNK, hh, d).transpose(0, 3, 1, 2, 4)


def _gdn_step(state, xs):
    value_c, kcd_c, qk_c, qdec_c, kdec_c, glast_c = xs
    v_new = value_c - jnp.einsum('bhcd,bhde->bhce', kcd_c, state)
    o = jnp.einsum('bhcd,bhde->bhce', qdec_c, state) + jnp.einsum('bhij,bhje->bhie', qk_c, v_new)
    state = state * glast_c[..., None, None] + jnp.einsum('bhcd,bhce->bhde', kdec_c, v_new)
    return state, o


def _gated_deltanet_mixer(q, k, v, z, beta_in, alpha_in, conv_w, a_log, dt_bias, norm_g):
    bsz, s, _ = q.shape
    nc = s // DN_CHUNK
    qkv = jax.nn.silu(_causal_dwconv(jnp.concatenate([q, k, v], axis=-1), conv_w)).astype(jnp.float32)
    q, k, v = jnp.split(qkv, 3, axis=-1)
    q = q.reshape(bsz, s, DN_HEADS, DN_HEAD_DIM)
    k = k.reshape(bsz, s, DN_HEADS, DN_HEAD_DIM)
    v = v.reshape(bsz, s, DN_HEADS, DN_HEAD_DIM)
    q = q * lax.rsqrt(jnp.sum(q * q, axis=-1, keepdims=True) + NORM_EPS) * (DN_HEAD_DIM ** -0.5)
    k = k * lax.rsqrt(jnp.sum(k * k, axis=-1, keepdims=True) + NORM_EPS)
    beta = jax.nn.sigmoid(beta_in.astype(jnp.float32))
    g = -jnp.exp(a_log.astype(jnp.float32)) * jax.nn.softplus(alpha_in.astype(jnp.float32) + dt_bias)
    qc, kc, vc = _to_chunks(q, nc), _to_chunks(k, nc), _to_chunks(v, nc)
    bc = _to_chunks(beta[..., None], nc)[..., 0]
    gam = jnp.cumsum(_to_chunks(g[..., None], nc)[..., 0], axis=-1)
    tri = jnp.tril(jnp.ones((DN_CHUNK, DN_CHUNK), bool))
    stri = jnp.tril(jnp.ones((DN_CHUNK, DN_CHUNK), bool), k=-1)
    lmask = jnp.exp(jnp.where(tri, gam[..., :, None] - gam[..., None, :], -jnp.inf))
    kb = kc * bc[..., None]
    m = jnp.where(stri, jnp.einsum('bhnid,bhnjd->bhnij', kb, kc) * lmask, 0.0)
    a_mat = jnp.eye(DN_CHUNK, dtype=jnp.float32) + m
    value = lax.linalg.triangular_solve(a_mat, vc * bc[..., None], left_side=True, lower=True)
    kcd = lax.linalg.triangular_solve(a_mat, kb * jnp.exp(gam)[..., None], left_side=True, lower=True)
    qk = jnp.einsum('bhnid,bhnjd->bhnij', qc, kc) * lmask
    qdec = qc * jnp.exp(gam)[..., None]
    kdec = kc * jnp.exp(gam[..., -1:] - gam)[..., None]
    glast = jnp.exp(gam[..., -1])
    xs = (jnp.moveaxis(value, 2, 0), jnp.moveaxis(kcd, 2, 0), jnp.moveaxis(qk, 2, 0),
          jnp.moveaxis(qdec, 2, 0), jnp.moveaxis(kdec, 2, 0), jnp.moveaxis(glast, 2, 0))
    state0 = jnp.zeros((bsz, DN_HEADS, DN_HEAD_DIM, DN_HEAD_DIM), jnp.float32)
    _, o = lax.scan(_gdn_step, state0, xs)
    o = o.transpose(1, 0, 3, 2, 4).reshape(bsz, s, DN_HEADS, DN_HEAD_DIM)
    o = o * lax.rsqrt(jnp.mean(o * o, axis=-1, keepdims=True) + NORM_EPS) * norm_g.astype(jnp.float32)
    o = o * jax.nn.silu(z.astype(jnp.float32).reshape(bsz, s, DN_HEADS, DN_HEAD_DIM))
    return o.reshape(bsz, s, DN_WIDTH).astype(z.dtype)


def _peer(x, wq, keys, pu, pv):
    bsz, s, d = x.shape
    t = bsz * s
    xf = x.reshape(t, d)
    q = (xf @ wq).astype(jnp.float32).reshape(t, PEER_HEADS, 2, PEER_QDIM // 2)
    sc = jnp.einsum('thpk,hpnk->thpn', q, keys.astype(jnp.float32))
    s1, i1 = lax.top_k(sc[:, :, 0], PEER_TOPK)
    s2, i2 = lax.top_k(sc[:, :, 1], PEER_TOPK)
    cand = (s1[..., :, None] + s2[..., None, :]).reshape(t, PEER_HEADS, PEER_TOPK * PEER_TOPK)
    cidx = (i1[..., :, None] * PEER_NKEYS + i2[..., None, :]).reshape(t, PEER_HEADS, PEER_TOPK * PEER_TOPK)
    top, pos = lax.top_k(cand, PEER_TOPK)
    idx = jnp.take_along_axis(cidx, pos, axis=-1)
    w = jax.nn.softmax(top, axis=-1)
    nb = t // PEER_BLOCK

    def block(args):
        xb, ib, wb = args
        ug = jnp.take(pu, ib, axis=0)
        act = jax.nn.gelu(jnp.einsum('pd,phkd->phk', xb, ug).astype(jnp.float32)) * wb
        vg = jnp.take(pv, ib, axis=0)
        return jnp.einsum('phk,phkd->pd', act.astype(xb.dtype), vg)

    y = lax.map(block, (xf.reshape(nb, PEER_BLOCK, d),
                        idx.reshape(nb, PEER_BLOCK, PEER_HEADS, PEER_TOPK),
                        w.reshape(nb, PEER_BLOCK, PEER_HEADS, PEER_TOPK)))
    return y.reshape(bsz, s, d)


def _layer(h, p_i, norm_mix, w_in, gm_ln_g, gm_ln_b, gm_ws, gm_bs,
           lru_conv_w, lru_conv_b, lru_wa, lru_ba, lru_wx, lru_bx, lru_lambda,
           dn_conv_w, dn_a_log, dn_dt_bias, dn_norm_g, w_out,
           norm_ffn, peer_wq, peer_keys, peer_u, peer_v, norm_ple, ple_wg, ple_wp):
    xn = _rms_norm(h, norm_mix)
    proj = xn @ w_in
    split_pts = [int(c) for c in np.cumsum(IN_SIZES)[:-1]]
    a_u, a_v, b_g, b_x, c_q, c_k, c_v, c_z, c_b, c_a = jnp.split(proj, split_pts, axis=-1)
    ya = _gmlp_mixer(a_u, a_v, gm_ln_g, gm_ln_b, gm_ws, gm_bs)
    yb = _rglru_mixer(b_g, b_x, lru_conv_w, lru_conv_b, lru_wa, lru_ba, lru_wx, lru_bx, lru_lambda)
    yc = _gated_deltanet_mixer(c_q, c_k, c_v, c_z, c_b, c_a, dn_conv_w, dn_a_log, dn_dt_bias, dn_norm_g)
    h = h + jnp.concatenate([ya, yb, yc], axis=-1) @ w_out
    h = h + _peer(_rms_norm(h, norm_ffn), peer_wq, peer_keys, peer_u, peer_v)
    h = h + jax.nn.sigmoid(_rms_norm(h, norm_ple) @ ple_wg) * (p_i @ ple_wp)
    return h


def setup_inputs(seed: int = 0) -> dict:
    key = jax.random.key(seed)
    ks = jax.random.split(key, 32)
    f32 = jnp.float32
    L, D = DEPTH, D_MODEL

    def nrm(k, shape, scale):
        return jax.random.normal(k, shape, f32) * scale

    u = jax.random.uniform(ks[14], (L, LRU_WIDTH), f32, 0.9, 0.999)
    sa = u ** (1.0 / LRU_C)
    dt = jnp.exp(jax.random.uniform(ks[17], (L, DN_HEADS), f32, math.log(1e-3), math.log(1e-1)))
    return {
        'x': nrm(ks[0], (BATCH, SEQ, D), 1.0),
        'p': nrm(ks[1], (DEPTH, BATCH, SEQ, PLE_DIM), 1.0),
        'norm_mix': 1.0 + nrm(ks[2], (L, D), 0.02),
        'w_in': nrm(ks[3], (L, D, IN_WIDTH), D ** -0.5),
        'gm_ln_g': 1.0 + nrm(ks[4], (L, GM_WIDTH), 0.02),
        'gm_ln_b': nrm(ks[5], (L, GM_WIDTH), 0.02),
        'gm_ws': nrm(ks[6], (L, GM_HEADS, GM_CHUNK, GM_CHUNK), GM_CHUNK ** -0.5),
        'gm_bs': 1.0 + nrm(ks[7], (L, GM_HEADS, GM_CHUNK), 0.02),
        'lru_conv_w': nrm(ks[8], (L, LRU_CONV, LRU_WIDTH), LRU_CONV ** -0.5),
        'lru_conv_b': nrm(ks[9], (L, LRU_WIDTH), 0.02),
        'lru_wa': nrm(ks[10], (L, LRU_HEADS, LRU_HEAD_DIM, LRU_HEAD_DIM), LRU_HEAD_DIM ** -0.5),
        'lru_ba': nrm(ks[11], (L, LRU_WIDTH), 0.02),
        'lru_wx': nrm(ks[12], (L, LRU_HEADS, LRU_HEAD_DIM, LRU_HEAD_DIM), LRU_HEAD_DIM ** -0.5),
        'lru_bx': nrm(ks[13], (L, LRU_WIDTH), 0.02),
        'lru_lambda': jnp.log(sa) - jnp.log1p(-sa),
        'dn_conv_w': nrm(ks[15], (L, DN_CONV, 3 * DN_WIDTH), DN_CONV ** -0.5),
        'dn_a_log': jnp.log(jax.random.uniform(ks[16], (L, DN_HEADS), f32, 1.0, 16.0)),
        'dn_dt_bias': dt + jnp.log(-jnp.expm1(-dt)),
        'dn_norm_g': 1.0 + nrm(ks[18], (L, DN_HEAD_DIM), 0.02),
        'w_out': nrm(ks[19], (L, D_MIX, D), D_MIX ** -0.5),
        'norm_ffn': 1.0 + nrm(ks[20], (L, D), 0.02),
        'peer_wq': nrm(ks[21], (L, D, PEER_HEADS * PEER_QDIM), D ** -0.5),
        'peer_keys': nrm(ks[22], (L, PEER_HEADS, 2, PEER_NKEYS, PEER_QDIM // 2), (PEER_QDIM // 2) ** -0.5),
        'peer_u': nrm(ks[23], (L, PEER_N, D), D ** -0.5),
        'peer_v': nrm(ks[24], (L, PEER_N, D), PEER_HEADS ** -0.5),
        'norm_ple': 1.0 + nrm(ks[25], (L, D), 0.02),
        'ple_wg': nrm(ks[26], (L, D, D), D ** -0.5),
        'ple_wp': nrm(ks[27], (L, PLE_DIM, D), PLE_DIM ** -0.5),
        'norm_final': 1.0 + nrm(ks[28], (D,), 0.02),
    }


def reference(x, p, norm_mix, w_in, gm_ln_g, gm_ln_b, gm_ws, gm_bs,
              lru_conv_w, lru_conv_b, lru_wa, lru_ba, lru_wx, lru_bx, lru_lambda,
              dn_conv_w, dn_a_log, dn_dt_bias, dn_norm_g, w_out,
              norm_ffn, peer_wq, peer_keys, peer_u, peer_v,
              norm_ple, ple_wg, ple_wp, norm_final):
    h = x
    for i in range(DEPTH):
        h = _layer(h, p[i], norm_mix[i], w_in[i], gm_ln_g[i], gm_ln_b[i], gm_ws[i], gm_bs[i],
                   lru_conv_w[i], lru_conv_b[i], lru_wa[i], lru_ba[i], lru_wx[i], lru_bx[i], lru_lambda[i],
                   dn_conv_w[i], dn_a_log[i], dn_dt_bias[i], dn_norm_g[i], w_out[i],
                   norm_ffn[i], peer_wq[i], peer_keys[i], peer_u[i], peer_v[i],
                   norm_ple[i], ple_wg[i], ple_wp[i])
    return _rms_norm(h, norm_final)
```

```python
import functools
import math

import jax
import jax.numpy as jnp
from jax import lax
from jax.experimental import pallas as pl
from jax.experimental.pallas import tpu as pltpu

F32 = jnp.float32
BF16 = jnp.bfloat16

NORM_EPS = 1e-6
LANES = 128
SUBLANES = 8
VMEM_LIMIT = 56 * 1024 * 1024

GM_HEADS, GM_HEAD_DIM, GM_CHUNK = 4, 64, 128
GM_WIDTH = GM_HEADS * GM_HEAD_DIM
LRU_HEADS, LRU_HEAD_DIM, LRU_CONV, LRU_C = 4, 64, 4, 8.0
LRU_WIDTH = LRU_HEADS * LRU_HEAD_DIM
DN_HEADS, DN_HEAD_DIM, DN_CONV = 4, 128, 4
DN_WIDTH = DN_HEADS * DN_HEAD_DIM
DN_BLOCK = 128
PEER_HEADS, PEER_NKEYS, PEER_TOPK = 8, 128, 16
PEER_HALF = 128

COL_GM = 0
COL_LRU = 512
COL_DN = 1024
COL_BETA = 3072
COL_ALPHA = 3200
PROJ_W = 3328


def _cparams(*sem):
    return pltpu.CompilerParams(dimension_semantics=sem, vmem_limit_bytes=VMEM_LIMIT)


def _rms(x, g):
    return x * lax.rsqrt(jnp.mean(x * x, axis=-1, keepdims=True) + NORM_EPS) * g


def _bdot(a, b):
    return jnp.dot(a.astype(BF16), b.astype(BF16), preferred_element_type=F32)


def _bdot_nt(a, b):
    return lax.dot_general(a.astype(BF16), b.astype(BF16), (((1,), (1,)), ((), ())),
                           preferred_element_type=F32)


def _split3_dot(a_exact, x):
    hi = x.astype(BF16)
    r1 = x - hi.astype(F32)
    mid = r1.astype(BF16)
    lo = (r1 - mid.astype(F32)).astype(BF16)
    a = a_exact.astype(BF16)
    return (jnp.dot(a, hi, preferred_element_type=F32) + jnp.dot(a, mid, preferred_element_type=F32)
            + jnp.dot(a, lo, preferred_element_type=F32))


def _softplus(x):
    return jnp.maximum(x, 0.0) + jnp.log1p(jnp.exp(-jnp.abs(x)))


def _in_proj_kernel(h_ref, g_ref, w_ref, o_ref):
    xn = _rms(h_ref[...], g_ref[...])
    o_ref[...] = jnp.dot(xn.astype(BF16), w_ref[...], preferred_element_type=F32)


def _in_proj(h, g, w, tt=256):
    t, d = h.shape
    return pl.pallas_call(
        _in_proj_kernel,
        grid=(t // tt,),
        in_specs=[pl.BlockSpec((tt, d), lambda i: (i, 0)),
                  pl.BlockSpec((1, d), lambda i: (0, 0)),
                  pl.BlockSpec((d, PROJ_W), lambda i: (0, 0))],
        out_specs=pl.BlockSpec((tt, PROJ_W), lambda i: (i, 0)),
        out_shape=jax.ShapeDtypeStruct((t, PROJ_W), F32),
        compiler_params=_cparams("parallel"),
        name="in_proj",
    )(h, g, w)


def _gmlp_kernel(uv_ref, lng_ref, lnb_ref, ws_ref, bs_ref, o_ref, *, n_chunks):
    row = lax.broadcasted_iota(jnp.int32, (GM_CHUNK, GM_CHUNK), 0)
    col = lax.broadcasted_iota(jnp.int32, (GM_CHUNK, GM_CHUNK), 1)
    lane_head = lax.broadcasted_iota(jnp.int32, (GM_CHUNK, GM_WIDTH), 1) // GM_HEAD_DIM
    ws = [jnp.where(row >= col, ws_ref[hh], 0.0).astype(BF16) for hh in range(GM_HEADS)]
    for c in range(n_chunks):
        r0 = c * GM_CHUNK
        u = jax.nn.gelu(uv_ref[r0:r0 + GM_CHUNK, 0:GM_WIDTH])
        v = jax.nn.gelu(uv_ref[r0:r0 + GM_CHUNK, GM_WIDTH:2 * GM_WIDTH])
        mu = jnp.mean(v, axis=-1, keepdims=True)
        vc = v - mu
        var = jnp.mean(vc * vc, axis=-1, keepdims=True)
        vn = (vc * lax.rsqrt(var + NORM_EPS) * lng_ref[...] + lnb_ref[...]).astype(BF16)
        sv = bs_ref[...]
        for hh in range(GM_HEADS):
            full = jnp.dot(ws[hh], vn, preferred_element_type=F32)
            sv = sv + jnp.where(lane_head == hh, full, 0.0)
        o_ref[r0:r0 + GM_CHUNK, :] = u * sv


def _gmlp(proj, lng, lnb, ws, bs_full, tt=512):
    t = proj.shape[0]
    return pl.pallas_call(
        functools.partial(_gmlp_kernel, n_chunks=tt // GM_CHUNK),
        grid=(t // tt,),
        in_specs=[pl.BlockSpec((tt, 2 * GM_WIDTH), lambda i: (i, COL_GM // (2 * GM_WIDTH))),
                  pl.BlockSpec((1, GM_WIDTH), lambda i: (0, 0)),
                  pl.BlockSpec((1, GM_WIDTH), lambda i: (0, 0)),
                  pl.BlockSpec((GM_HEADS, GM_CHUNK, GM_CHUNK), lambda i: (0, 0, 0)),
                  pl.BlockSpec((GM_CHUNK, GM_WIDTH), lambda i: (0, 0))],
        out_specs=pl.BlockSpec((tt, GM_WIDTH), lambda i: (i, 0)),
        out_shape=jax.ShapeDtypeStruct((t, GM_WIDTH), F32),
        compiler_params=_cparams("parallel"),
        name="gmlp",
    )(proj, lng, lnb, ws, bs_full)


def _lru_kernel(gx_ref, cw_ref, cb_ref, wa_ref, ba_ref, wx_ref, bx_ref, lam_ref, o_ref,
                xbuf_ref, hprev_ref, *, tt):
    c = pl.program_id(1)

    @pl.when(c == 0)
    def _():
        xbuf_ref[0:SUBLANES, :] = jnp.zeros((SUBLANES, LRU_WIDTH), F32)
        hprev_ref[...] = jnp.zeros_like(hprev_ref)

    xbuf_ref[SUBLANES:SUBLANES + tt, :] = gx_ref[:, LRU_WIDTH:2 * LRU_WIDTH]
    xc = cb_ref[...]
    for k in range(LRU_CONV):
        off = SUBLANES - (LRU_CONV - 1) + k
        xc = xc + xbuf_ref[off:off + tt, :] * cw_ref[k:k + 1, :]
    xbuf_ref[0:SUBLANES, :] = xbuf_ref[tt:tt + SUBLANES, :]

    r = jax.nn.sigmoid(_bdot(xc, wa_ref[...]) + ba_ref[...])
    i = jax.nn.sigmoid(_bdot(xc, wx_ref[...]) + bx_ref[...])
    log_a = (-LRU_C) * r * _softplus(-lam_ref[...])
    a = jnp.exp(log_a)
    th = jnp.tanh(log_a)
    mult = jnp.sqrt(-2.0 * th / (1.0 - th))
    trow = lax.broadcasted_iota(jnp.int32, (tt, LRU_WIDTH), 0)
    first_row = jnp.where(c == 0, 0, -1)
    mult = jnp.where(trow == first_row, 1.0, mult)
    b = mult * i * xc

    d = 1
    while d < tt:
        a_sh = jnp.where(trow >= d, pltpu.roll(a, d, 0), 1.0)
        b_sh = jnp.where(trow >= d, pltpu.roll(b, d, 0), 0.0)
        b = a * b_sh + b
        a = a * a_sh
        d *= 2
    h = a * hprev_ref[0:1, :] + b
    hprev_ref[...] = jnp.broadcast_to(h[tt - 1:tt, :], hprev_ref.shape)
    o_ref[...] = h * jax.nn.gelu(gx_ref[:, 0:LRU_WIDTH])


def _lru(proj, bsz, seq, cw, cb, wa_bd, ba, wx_bd, bx, lam, tt=256):
    t = proj.shape[0]
    nct = seq // tt
    small = lambda shape: pl.BlockSpec(shape, lambda b, c: (0,) * len(shape))
    return pl.pallas_call(
        functools.partial(_lru_kernel, tt=tt),
        grid=(bsz, nct),
        in_specs=[pl.BlockSpec((tt, 2 * LRU_WIDTH), lambda b, c: (b * nct + c, COL_LRU // (2 * LRU_WIDTH))),
                  small((LRU_CONV, LRU_WIDTH)), small((1, LRU_WIDTH)),
                  small((LRU_WIDTH, LRU_WIDTH)), small((1, LRU_WIDTH)),
                  small((LRU_WIDTH, LRU_WIDTH)), small((1, LRU_WIDTH)),
                  small((1, LRU_WIDTH))],
        out_specs=pl.BlockSpec((tt, LRU_WIDTH), lambda b, c: (b * nct + c, 0)),
        out_shape=jax.ShapeDtypeStruct((t, LRU_WIDTH), F32),
        scratch_shapes=[pltpu.VMEM((tt + SUBLANES, LRU_WIDTH), F32),
                        pltpu.VMEM((SUBLANES, LRU_WIDTH), F32)],
        compiler_params=_cparams("parallel", "arbitrary"),
        name="rglru",
    )(proj, cw, cb, wa_bd, ba, wx_bd, bx, lam)


def _gdn_kernel(q_ref, k_ref, v_ref, z_ref, bin_ref, ain_ref, cw_ref, alog_ref, dtb_ref, ng_ref,
                o_ref, xbuf_ref, state_ref):
    c = pl.program_id(1)
    n = DN_BLOCK

    @pl.when(c == 0)
    def _():
        for idx in range(3):
            xbuf_ref[idx, 0:SUBLANES, :] = jnp.zeros((SUBLANES, DN_WIDTH), F32)
        state_ref[...] = jnp.zeros_like(state_ref)

    def conv_silu(idx, x_ref):
        xbuf_ref[idx, SUBLANES:SUBLANES + n, :] = x_ref[...]
        acc = jnp.zeros((n, DN_WIDTH), F32)
        for kk in range(DN_CONV):
            off = SUBLANES - (DN_CONV - 1) + kk
            acc = acc + xbuf_ref[idx, off:off + n, :] * cw_ref[kk:kk + 1, idx * DN_WIDTH:(idx + 1) * DN_WIDTH]
        xbuf_ref[idx, 0:SUBLANES, :] = xbuf_ref[idx, n:n + SUBLANES, :]
        return acc * jax.nn.sigmoid(acc)

    q_all = conv_silu(0, q_ref)
    k_all = conv_silu(1, k_ref)
    v_all = conv_silu(2, v_ref)

    row = lax.broadcasted_iota(jnp.int32, (n, n), 0)
    col = lax.broadcasted_iota(jnp.int32, (n, n), 1)
    lower_incl = row >= col
    lower_strict = row > col
    eye = jnp.where(row == col, 1.0, 0.0)

    beta_all = jax.nn.sigmoid(bin_ref[...])
    g_all = -jnp.exp(alog_ref[...]) * _softplus(ain_ref[...] + dtb_ref[...])
    gam_all = _split3_dot(jnp.where(lower_incl, 1.0, 0.0), g_all)
    gam_all_t = gam_all.T

    for hh in range(DN_HEADS):
        sl = slice(hh * DN_HEAD_DIM, (hh + 1) * DN_HEAD_DIM)
        q = q_all[:, sl]
        k = k_all[:, sl]
        v = v_all[:, sl]
        q = q * lax.rsqrt(jnp.sum(q * q, axis=-1, keepdims=True) + NORM_EPS) * (DN_HEAD_DIM ** -0.5)
        k = k * lax.rsqrt(jnp.sum(k * k, axis=-1, keepdims=True) + NORM_EPS)
        beta = beta_all[:, hh:hh + 1]
        gam_c = gam_all[:, hh:hh + 1]
        gam_r = gam_all_t[hh:hh + 1, :]
        glast = gam_all[n - 1:n, hh:hh + 1]
        egam = jnp.exp(gam_c)
        decay = jnp.where(lower_incl, jnp.exp(jnp.minimum(gam_c - gam_r, 0.0)), 0.0)
        kb = k * beta
        kq = _bdot_nt(jnp.concatenate([kb, q], axis=0), k)
        nmat = jnp.where(lower_strict, -(kq[0:n] * decay), 0.0)
        qk = kq[n:2 * n] * decay
        inv = eye + nmat
        npow = nmat
        steps = int(math.log2(n)) - 1
        for _ in range(steps):
            npow = _bdot(npow, npow)
            inv = inv + _bdot(inv, npow)
        solved = _bdot(inv, jnp.concatenate([v * beta, kb * egam], axis=1))
        value = solved[:, 0:DN_HEAD_DIM]
        kcd = solved[:, DN_HEAD_DIM:2 * DN_HEAD_DIM]
        qdec = q * egam
        kdec = k * jnp.exp(glast - gam_c)
        state = state_ref[hh]
        proj_s = _bdot(jnp.concatenate([kcd, qdec], axis=0), state)
        v_new = value - proj_s[0:n]
        o = proj_s[n:2 * n] + _bdot(qk, v_new)
        state_ref[hh] = state * jnp.exp(glast) + _bdot(kdec.T, v_new)
        o = o * lax.rsqrt(jnp.mean(o * o, axis=-1, keepdims=True) + NORM_EPS) * ng_ref[...]
        zz = z_ref[:, sl]
        o_ref[:, sl] = o * (zz * jax.nn.sigmoid(zz))


def _gdn(proj, bsz, seq, cw, alog, dtb, ng):
    t = proj.shape[0]
    n = DN_BLOCK
    nb = seq // n
    tok = lambda colblk: (lambda b, c: (b * nb + c, colblk))
    small = lambda shape: pl.BlockSpec(shape, lambda b, c: (0,) * len(shape))
    return pl.pallas_call(
        _gdn_kernel,
        grid=(bsz, nb),
        in_specs=[pl.BlockSpec((n, DN_WIDTH), tok(COL_DN // DN_WIDTH)),
                  pl.BlockSpec((n, DN_WIDTH), tok(COL_DN // DN_WIDTH + 1)),
                  pl.BlockSpec((n, DN_WIDTH), tok(COL_DN // DN_WIDTH + 2)),
                  pl.BlockSpec((n, DN_WIDTH), tok(COL_DN // DN_WIDTH + 3)),
                  pl.BlockSpec((n, LANES), tok(COL_BETA // LANES)),
                  pl.BlockSpec((n, LANES), tok(COL_ALPHA // LANES)),
                  small((DN_CONV, 3 * DN_WIDTH)), small((1, LANES)), small((1, LANES)),
                  small((1, DN_HEAD_DIM))],
        out_specs=pl.BlockSpec((n, DN_WIDTH), lambda b, c: (b * nb + c, 0)),
        out_shape=jax.ShapeDtypeStruct((t, DN_WIDTH), F32),
        scratch_shapes=[pltpu.VMEM((3, n + SUBLANES, DN_WIDTH), F32),
                        pltpu.VMEM((DN_HEADS, DN_HEAD_DIM, DN_HEAD_DIM), F32)],
        compiler_params=_cparams("parallel", "arbitrary"),
        name="gdn",
    )(proj, proj, proj, proj, proj, proj, cw, alog, dtb, ng)


def _out_proj_kernel(h_ref, ya_ref, yb_ref, yc_ref, wa_ref, wb_ref, wc_ref, o_ref):
    o_ref[...] = (h_ref[...] + _bdot(ya_ref[...], wa_ref[...]) + _bdot(yb_ref[...], wb_ref[...])
                  + _bdot(yc_ref[...], wc_ref[...]))


def _out_proj(h, ya, yb, yc, wa, wb, wc, tt=512):
    t, d = h.shape
    tok = lambda w: pl.BlockSpec((tt, w), lambda i: (i, 0))
    full = lambda a: pl.BlockSpec(a.shape, lambda i: (0, 0))
    return pl.pallas_call(
        _out_proj_kernel,
        grid=(t // tt,),
        in_specs=[tok(d), tok(GM_WIDTH), tok(LRU_WIDTH), tok(DN_WIDTH), full(wa), full(wb), full(wc)],
        out_specs=tok(d),
        out_shape=jax.ShapeDtypeStruct((t, d), F32),
        compiler_params=_cparams("parallel"),
        name="out_proj",
    )(h, ya, yb, yc, wa, wb, wc)


N_EXTRACT = PEER_TOPK + 1


def _extract_top(x, count):
    vals = []
    for _ in range(count):
        m = jnp.max(x, axis=0, keepdims=True)
        vals.append(m)
        x = jnp.where(x == m, -jnp.inf, x)
    return vals


def _route_kernel(h_ref, g_ref, wq_ref, keys_ref, xn_ref, e1_ref, t1_ref, s2_ref, e2_ref, sc_ref, *, tt):
    xn = _rms(h_ref[...], g_ref[...]).astype(BF16)
    xn_ref[...] = xn
    for hp in range(2 * PEER_HEADS):
        q_hp = jnp.dot(xn, wq_ref[:, hp * PEER_HALF:(hp + 1) * PEER_HALF], preferred_element_type=F32)
        sc_ref[hp] = _bdot_nt(keys_ref[hp], q_hp)

    neg = jnp.full((SUBLANES - 1, tt), -jnp.inf, F32)

    def head_body(hh, carry):
        s1 = sc_ref[2 * hh]
        s2 = sc_ref[2 * hh + 1]
        v1 = _extract_top(s1, N_EXTRACT)
        v2 = _extract_top(s2, N_EXTRACT)
        v2_all = jnp.concatenate(v2 + [neg], axis=0)
        v2_top8 = v2_all[0:SUBLANES]
        blocks = [v1[0] + v2_all]
        for a in range(1, SUBLANES):
            blocks.append(v1[a] + v2_top8)
        blocks.append(jnp.concatenate(v1[SUBLANES:N_EXTRACT] + [neg], axis=0) + v2[0])
        cand = jnp.concatenate(blocks, axis=0)
        m = _extract_top(cand, N_EXTRACT)
        tau = 0.5 * (m[PEER_TOPK - 1] + m[PEER_TOPK])
        zsum = jnp.ones_like(m[0])
        for kk in range(1, PEER_TOPK):
            zsum = zsum + jnp.exp(m[kk] - m[0])
        e1_ref[hh] = jnp.exp(s1 - v1[0]) / zsum
        t1_ref[hh] = tau - s1
        s2_ref[hh] = s2
        e2_ref[hh] = jnp.exp(s2 - v2[0])
        return carry

    lax.fori_loop(0, PEER_HEADS, head_body, 0)


def _route(h, g, wq, keys, tt=256):
    t, d = h.shape
    hk = pl.BlockSpec((PEER_HEADS, PEER_NKEYS, tt), lambda i: (0, 0, i))
    hk_shape = jax.ShapeDtypeStruct((PEER_HEADS, PEER_NKEYS, t), F32)
    return pl.pallas_call(
        functools.partial(_route_kernel, tt=tt),
        grid=(t // tt,),
        in_specs=[pl.BlockSpec((tt, d), lambda i: (i, 0)),
                  pl.BlockSpec((1, d), lambda i: (0, 0)),
                  pl.BlockSpec(wq.shape, lambda i: (0, 0)),
                  pl.BlockSpec(keys.shape, lambda i: (0, 0, 0))],
        out_specs=[pl.BlockSpec((tt, d), lambda i: (i, 0)), hk, hk, hk, hk],
        out_shape=[jax.ShapeDtypeStruct((t, d), BF16), hk_shape, hk_shape, hk_shape, hk_shape],
        scratch_shapes=[pltpu.VMEM((2 * PEER_HEADS, PEER_NKEYS, tt), F32)],
        compiler_params=_cparams("parallel"),
        name="peer_route",
    )(h, g, wq, keys)


def _peer_dense_kernel(xn_ref, pu_ref, pvt_ref, e1_ref, t1_ref, s2_ref, e2_ref, h_ref, o_ref,
                       acc_ref, z_ref, gw_ref, *, n_sub):
    k = pl.program_id(1)

    @pl.when(k == 0)
    def _():
        acc_ref[...] = jnp.zeros_like(acc_ref)

    z_ref[...] = lax.dot_general(pu_ref[...], xn_ref[...], (((1,), (1,)), ((), ())),
                                 preferred_element_type=F32)

    def sub_body(ii, carry):
        r0 = pl.multiple_of(ii * PEER_NKEYS, PEER_NKEYS)
        w = None
        for hh in range(PEER_HEADS):
            t1 = t1_ref[hh, pl.ds(ii, 1), :]
            e1 = e1_ref[hh, pl.ds(ii, 1), :]
            term = jnp.where(s2_ref[hh] > t1, e2_ref[hh], 0.0) * e1
            w = term if w is None else w + term
        gw_ref[pl.ds(r0, PEER_NKEYS), :] = (jax.nn.gelu(z_ref[pl.ds(r0, PEER_NKEYS), :]) * w).astype(BF16)
        return carry

    lax.fori_loop(0, n_sub, sub_body, 0)
    acc_ref[...] += jnp.dot(pvt_ref[...], gw_ref[...], preferred_element_type=F32)

    @pl.when(k == pl.num_programs(1) - 1)
    def _():
        o_ref[...] = h_ref[...] + acc_ref[...].T


def _peer_dense(xn, pu, pvt, e1, t1, s2, e2, h, tt=512, nt=1024):
    t, d = h.shape
    ne = pu.shape[0]
    n_sub = nt // PEER_NKEYS
    sub = pl.BlockSpec((PEER_HEADS, n_sub, tt), lambda i, k: (0, k, i))
    allk = pl.BlockSpec((PEER_HEADS, PEER_NKEYS, tt), lambda i, k: (0, 0, i))
    return pl.pallas_call(
        functools.partial(_peer_dense_kernel, n_sub=n_sub),
        grid=(t // tt, ne // nt),
        in_specs=[pl.BlockSpec((tt, d), lambda i, k: (i, 0)),
                  pl.BlockSpec((nt, d), lambda i, k: (k, 0)),
                  pl.BlockSpec((d, nt), lambda i, k: (0, k)),
                  sub, sub, allk, allk,
                  pl.BlockSpec((tt, d), lambda i, k: (i, 0))],
        out_specs=pl.BlockSpec((tt, d), lambda i, k: (i, 0)),
        out_shape=jax.ShapeDtypeStruct((t, d), F32),
        scratch_shapes=[pltpu.VMEM((d, tt), F32), pltpu.VMEM((nt, tt), F32), pltpu.VMEM((nt, tt), BF16)],
        compiler_params=_cparams("parallel", "arbitrary"),
        name="peer_dense",
    )(xn, pu, pvt, e1, t1, s2, e2, h)


def _ple_kernel(h_ref, p_ref, g_ref, wg_ref, wp_ref, o_ref):
    h = h_ref[...]
    gate = jax.nn.sigmoid(_bdot(_rms(h, g_ref[...]), wg_ref[...]))
    o_ref[...] = h + gate * _bdot(p_ref[...], wp_ref[...])


def _ple(h, p, g, wg, wp, tt=512):
    t, d = h.shape
    pd = p.shape[1]
    return pl.pallas_call(
        _ple_kernel,
        grid=(t // tt,),
        in_specs=[pl.BlockSpec((tt, d), lambda i: (i, 0)),
                  pl.BlockSpec((tt, pd), lambda i: (i, 0)),
                  pl.BlockSpec((1, d), lambda i: (0, 0)),
                  pl.BlockSpec((d, d), lambda i: (0, 0)),
                  pl.BlockSpec((pd, d), lambda i: (0, 0))],
        out_specs=pl.BlockSpec((tt, d), lambda i: (i, 0)),
        out_shape=jax.ShapeDtypeStruct((t, d), F32),
        compiler_params=_cparams("parallel"),
        name="ple",
    )(h, p, g, wg, wp)


def _final_norm_kernel(h_ref, g_ref, o_ref):
    o_ref[...] = _rms(h_ref[...], g_ref[...])


def _final_norm(h, g, tt=512):
    t, d = h.shape
    return pl.pallas_call(
        _final_norm_kernel,
        grid=(t // tt,),
        in_specs=[pl.BlockSpec((tt, d), lambda i: (i, 0)), pl.BlockSpec((1, d), lambda i: (0, 0))],
        out_specs=pl.BlockSpec((tt, d), lambda i: (i, 0)),
        out_shape=jax.ShapeDtypeStruct((t, d), F32),
        compiler_params=_cparams("parallel"),
        name="final_norm",
    )(h, g)


def _pad_cols(a, width):
    return jnp.pad(a, ((0, 0), (0, width - a.shape[1])))


def _block_diag(w):
    hh, d, _ = w.shape
    out = jnp.zeros((hh * d, hh * d), w.dtype)
    for i in range(hh):
        out = lax.dynamic_update_slice(out, w[i], (i * d, i * d))
    return out


def _prep_w_in(w_in):
    main = w_in[:, 0:COL_BETA]
    beta = _pad_cols(w_in[:, COL_BETA:COL_BETA + DN_HEADS], LANES)
    alpha = _pad_cols(w_in[:, COL_BETA + DN_HEADS:COL_BETA + 2 * DN_HEADS], LANES)
    return jnp.concatenate([main, beta, alpha], axis=1).astype(BF16)


def _layer(h, p_i, bsz, seq, norm_mix, w_in, gm_ln_g, gm_ln_b, gm_ws, gm_bs,
           lru_conv_w, lru_conv_b, lru_wa, lru_ba, lru_wx, lru_bx, lru_lambda,
           dn_conv_w, dn_a_log, dn_dt_bias, dn_norm_g, w_out,
           norm_ffn, peer_wq, peer_keys, peer_u, peer_v, norm_ple, ple_wg, ple_wp):
    row = lambda a: a.reshape(1, -1)
    proj = _in_proj(h, row(norm_mix), _prep_w_in(w_in))
    bs_full = jnp.repeat(gm_bs.T, GM_HEAD_DIM, axis=1)
    ya = _gmlp(proj, row(gm_ln_g), row(gm_ln_b), gm_ws, bs_full)
    yb = _lru(proj, bsz, seq, lru_conv_w, row(lru_conv_b), _block_diag(lru_wa).astype(BF16), row(lru_ba),
              _block_diag(lru_wx).astype(BF16), row(lru_bx), row(lru_lambda))
    yc = _gdn(proj, bsz, seq, dn_conv_w, _pad_cols(row(dn_a_log), LANES), _pad_cols(row(dn_dt_bias), LANES),
              row(dn_norm_g))
    w_out_b = w_out.astype(BF16)
    h = _out_proj(h, ya, yb, yc, w_out_b[0:GM_WIDTH], w_out_b[GM_WIDTH:GM_WIDTH + LRU_WIDTH],
                  w_out_b[GM_WIDTH + LRU_WIDTH:])
    keys = peer_keys.reshape(2 * PEER_HEADS, PEER_NKEYS, PEER_HALF).astype(BF16)
    xn, e1, t1, s2, e2 = _route(h, row(norm_ffn), peer_wq.astype(BF16), keys)
    h = _peer_dense(xn, peer_u.astype(BF16), peer_v.T.astype(BF16), e1, t1, s2, e2, h)
    h = _ple(h, p_i, row(norm_ple), ple_wg.astype(BF16), ple_wp.astype(BF16))
    return h


def kernel(x, p, norm_mix, w_in, gm_ln_g, gm_ln_b, gm_ws, gm_bs, lru_conv_w, lru_conv_b, lru_wa, lru_ba,
           lru_wx, lru_bx, lru_lambda, dn_conv_w, dn_a_log, dn_dt_bias, dn_norm_g, w_out, norm_ffn, peer_wq,
           peer_keys, peer_u, peer_v, norm_ple, ple_wg, ple_wp, norm_final):
    bsz, seq, d = x.shape
    depth = p.shape[0]
    h = x.reshape(bsz * seq, d)
    for i in range(depth):
        h = _layer(h, p[i].reshape(bsz * seq, -1), bsz, seq, norm_mix[i], w_in[i], gm_ln_g[i], gm_ln_b[i],
                   gm_ws[i], gm_bs[i], lru_conv_w[i], lru_conv_b[i], lru_wa[i], lru_ba[i], lru_wx[i],
                   lru_bx[i], lru_lambda[i], dn_conv_w[i], dn_a_log[i], dn_dt_bias[i], dn_norm_g[i],
                   w_out[i], norm_ffn[i], peer_wq[i], peer_keys[i], peer_u[i], peer_v[i], norm_ple[i],
                   ple_wg[i], ple_wp[i])
    return _final_norm(h, norm_final.reshape(1, -1)).reshape(bsz, seq, d)
```
